```python
import jax, jax.numpy as jnp
from jax import lax
import numpy as np

D_MODEL = 1024
BATCH = 16
SEQ = 2048
DEPTH = 2

CHUNK = 64
MEM_LEN = 256
SGU_BLOCK = 128
A_GROUPS = 4
A_WIDTH = D_MODEL // 2
A_GROUP_DIM = A_WIDTH // A_GROUPS
POOL_WINDOWS = (2, 4, 8, 16)
B_WIDTH = D_MODEL // 2
B_GROUP_DIM = B_WIDTH // len(POOL_WINDOWS)
C_WIDTH = D_MODEL
CONV_K = 3
XATTN_HEADS = 4
XATTN_HEAD_DIM = D_MODEL // XATTN_HEADS
D_FF = ((8 * D_MODEL // 3 + 127) // 128) * 128
FFN_CONV_K = 3
LN_EPS = 1e-5
DEEPNORM_ALPHA = (2 * DEPTH) ** 0.25
DEEPNORM_BETA = (8 * DEPTH) ** -0.25
N_EVEN = (DEPTH + 1) // 2
N_ODD = DEPTH // 2

kernel_name = "hybrid_sgu_pool_shortconv_deepnorm_trunk"


def layer_norm(x, g, b):
    xf = x.astype(jnp.float32)
    mu = jnp.mean(xf, axis=-1, keepdims=True)
    var = jnp.mean(jnp.square(xf - mu), axis=-1, keepdims=True)
    y = (xf - mu) * lax.rsqrt(var + LN_EPS)
    return (y * g.astype(jnp.float32) + b.astype(jnp.float32)).astype(x.dtype)


def causal_dwconv(x, w):
    k, c = w.shape
    return lax.conv_general_dilated(
        x, w[:, None, :].astype(x.dtype), window_strides=(1,), padding=[(k - 1, 0)],
        dimension_numbers=('NWC', 'WIO', 'NWC'), feature_group_count=c)


def trailing_mean_minus_self(v, window):
    seq = v.shape[1]
    vf = v.astype(jnp.float32)
    cs = jnp.cumsum(vf, axis=1)
    lagged = jnp.pad(cs, ((0, 0), (window, 0), (0, 0)))[:, :seq]
    count = jnp.minimum(jnp.arange(1, seq + 1), window).astype(jnp.float32)
    mean = (cs - lagged) / count[None, :, None]
    return (mean - vf).astype(v.dtype)


def mixer_ab(x, w_in, ln_g, ln_b, w_s, b_s, pool_w, pool_scale, w_out):
    bsz, seq, _ = x.shape
    proj = x @ w_in
    uv = jax.nn.gelu(proj[..., :2 * A_WIDTH])
    u, v = uv[..., :A_WIDTH], uv[..., A_WIDTH:]
    v = layer_norm(v, ln_g, ln_b)
    chunk_id = jnp.arange(SGU_BLOCK) // CHUNK
    mask = chunk_id[None, :] <= chunk_id[:, None]
    w_m = jnp.where(mask[None], w_s, 0.0).astype(v.dtype)
    vb = v.reshape(bsz, seq // SGU_BLOCK, SGU_BLOCK, A_GROUPS, A_GROUP_DIM)
    sp = jnp.einsum('gij,bnjgc->bnigc', w_m, vb) + b_s.T[None, None, :, :, None]
    y_a = u * sp.reshape(bsz, seq, A_WIDTH)
    xb = proj[..., 2 * A_WIDTH:].reshape(bsz, seq, len(POOL_WINDOWS), B_GROUP_DIM)
    pooled = jnp.stack([trailing_mean_minus_self(xb[:, :, g], w)
                        for g, w in enumerate(POOL_WINDOWS)], axis=2)
    y_b = jnp.einsum('bsgc,gcd->bsgd', pooled, pool_w).reshape(bsz, seq, B_WIDTH) * pool_scale
    return jnp.concatenate([y_a, y_b], axis=-1) @ w_out


def mixer_c(x, w_in, conv_w, w_out):
    proj = x @ w_in
    gate_b, gate_c, h = jnp.split(proj, 3, axis=-1)
    y = gate_b * causal_dwconv(gate_c * h, conv_w)
    return y @ w_out


def cross_attend(x, mem, wq, wkv, wo):
    bsz, seq, d = x.shape
    q = (x @ wq).reshape(bsz, seq, XATTN_HEADS, XATTN_HEAD_DIM)
    kv = mem @ wkv
    k = kv[..., :d].reshape(bsz, -1, XATTN_HEADS, XATTN_HEAD_DIM)
    v = kv[..., d:].reshape(bsz, -1, XATTN_HEADS, XATTN_HEAD_DIM)
    scores = jnp.einsum('bshd,bmhd->bhsm', q.astype(jnp.float32), k.astype(jnp.float32))
    probs = jax.nn.softmax(scores * (XATTN_HEAD_DIM ** -0.5), axis=-1).astype(x.dtype)
    out = jnp.einsum('bhsm,bmhd->bshd', probs, v).reshape(bsz, seq, d)
    return out @ wo


def conv_ffn(x, w_up, conv_w, conv_b, w_down):
    up = x @ w_up
    a, g = up[..., :D_FF], up[..., D_FF:]
    g = causal_dwconv(g, conv_w) + conv_b
    return (jax.nn.gelu(g) * a) @ w_down


def _normal(k, shape, scale):
    return jax.random.normal(k, shape, jnp.float32) * scale


def setup_inputs(seed: int = 0) -> dict:
    key = jax.random.key(seed)
    ks = iter(jax.random.split(key, 32))
    d = D_MODEL
    beta = DEEPNORM_BETA
    x = _normal(next(ks), (BATCH, SEQ, d), 1.0)
    mem = _normal(next(ks), (BATCH, MEM_LEN, d), 1.0)
    ab_w_in = _normal(next(ks), (N_EVEN, d, 2 * A_WIDTH + B_WIDTH), d ** -0.5)
    sgu_ln_g = 1.0 + _normal(next(ks), (N_EVEN, A_WIDTH), 0.05)
    sgu_ln_b = _normal(next(ks), (N_EVEN, A_WIDTH), 0.02)
    sgu_w = _normal(next(ks), (N_EVEN, A_GROUPS, SGU_BLOCK, SGU_BLOCK), SGU_BLOCK ** -0.5)
    sgu_b = 1.0 + _normal(next(ks), (N_EVEN, A_GROUPS, SGU_BLOCK), 0.1)
    pool_w = _normal(next(ks), (N_EVEN, len(POOL_WINDOWS), B_GROUP_DIM, B_GROUP_DIM), B_GROUP_DIM ** -0.5)
    pool_scale = 1.0 + _normal(next(ks), (N_EVEN, B_WIDTH), 0.1)
    ab_w_out = _normal(next(ks), (N_EVEN, A_WIDTH + B_WIDTH, d), (A_WIDTH + B_WIDTH) ** -0.5 * beta)
    c_w_in = _normal(next(ks), (N_ODD, d, 3 * C_WIDTH), d ** -0.5)
    c_conv_w = _normal(next(ks), (N_ODD, CONV_K, C_WIDTH), CONV_K ** -0.5)
    c_w_out = _normal(next(ks), (N_ODD, C_WIDTH, d), C_WIDTH ** -0.5 * beta)
    ln_mix_g = 1.0 + _normal(next(ks), (DEPTH, d), 0.05)
    ln_mix_b = _normal(next(ks), (DEPTH, d), 0.02)
    xa_wq = _normal(next(ks), (DEPTH, d, d), d ** -0.5)
    xa_wk = _normal(next(ks), (DEPTH, d, d), d ** -0.5)
    xa_wv = _normal(next(ks), (DEPTH, d, d), d ** -0.5 * beta)
    xa_wkv = jnp.concatenate([xa_wk, xa_wv], axis=-1)
    xa_wo = _normal(next(ks), (DEPTH, d, d), d ** -0.5 * beta)
    ln_xa_g = 1.0 + _normal(next(ks), (DEPTH, d), 0.05)
    ln_xa_b = _normal(next(ks), (DEPTH, d), 0.02)
    ffn_w_up = _normal(next(ks), (DEPTH, d, 2 * D_FF), d ** -0.5)
    ffn_conv_w = _normal(next(ks), (DEPTH, FFN_CONV_K, D_FF), FFN_CONV_K ** -0.5)
    ffn_conv_b = _normal(next(ks), (DEPTH, D_FF), 0.02)
    ffn_w_down = _normal(next(ks), (DEPTH, D_FF, d), D_FF ** -0.5 * beta)
    ln_ffn_g = 1.0 + _normal(next(ks), (DEPTH, d), 0.05)
    ln_ffn_b = _normal(next(ks), (DEPTH, d), 0.02)
    return {"x": x, "mem": mem, "ab_w_in": ab_w_in, "sgu_ln_g": sgu_ln_g, "sgu_ln_b": sgu_ln_b,
            "sgu_w": sgu_w, "sgu_b": sgu_b, "pool_w": pool_w, "pool_scale": pool_scale,
            "ab_w_out": ab_w_out, "c_w_in": c_w_in, "c_conv_w": c_conv_w, "c_w_out": c_w_out,
            "ln_mix_g": ln_mix_g, "ln_mix_b": ln_mix_b, "xa_wq": xa_wq, "xa_wkv": xa_wkv,
            "xa_wo": xa_wo, "ln_xa_g": ln_xa_g, "ln_xa_b": ln_xa_b, "ffn_w_up": ffn_w_up,
            "ffn_conv_w": ffn_conv_w, "ffn_conv_b": ffn_conv_b, "ffn_w_down": ffn_w_down,
            "ln_ffn_g": ln_ffn_g, "ln_ffn_b": ln_ffn_b}


def reference(x, mem, ab_w_in, sgu_ln_g, sgu_ln_b, sgu_w, sgu_b, pool_w, pool_scale,
              ab_w_out, c_w_in, c_conv_w, c_w_out, ln_mix_g, ln_mix_b, xa_wq, xa_wkv,
              xa_wo, ln_xa_g, ln_xa_b, ffn_w_up, ffn_conv_w, ffn_conv_b, ffn_w_down,
              ln_ffn_g, ln_ffn_b):
    alpha = DEEPNORM_ALPHA
    for layer in range(DEPTH):
        j = layer // 2
        if layer % 2 == 0:
            mix = mixer_ab(x, ab_w_in[j], sgu_ln_g[j], sgu_ln_b[j], sgu_w[j], sgu_b[j],
                           pool_w[j], pool_scale[j], ab_w_out[j])
        else:
            mix = mixer_c(x, c_w_in[j], c_conv_w[j], c_w_out[j])
        x = layer_norm(alpha * x + mix, ln_mix_g[layer], ln_mix_b[layer])
        x = layer_norm(alpha * x + cross_attend(x, mem, xa_wq[layer], xa_wkv[layer], xa_wo[layer]),
                       ln_xa_g[layer], ln_xa_b[layer])
        x = layer_norm(alpha * x + conv_ffn(x, ffn_w_up[layer], ffn_conv_w[layer],
                                            ffn_conv_b[layer], ffn_w_down[layer]),
                       ln_ffn_g[layer], ln_ffn_b[layer])
    return x
```

```python
import functools

import jax
import jax.numpy as jnp
from jax import lax
from jax.experimental import pallas as pl
from jax.experimental.pallas import tpu as pltpu

LN_EPS = 1e-5
CHUNK = 64
SGU_BLOCK = 128
POOL_WINDOWS = (2, 4, 8, 16)
XATTN_HEADS = 4
CONV_TAPS = 3

SUBLANES = 8
MXU_COLUMNS = 256
ROW_TILE = 512
VMEM_LIMIT_BYTES = 56 * 1024 * 1024

_F32 = jnp.float32
_BF16 = jnp.bfloat16


def _dot(a, b):
    return jnp.dot(a, b, preferred_element_type=_F32)


def _layer_norm(y, g, b):
    mu = jnp.mean(y, axis=-1, keepdims=True)
    yc = y - mu
    var = jnp.mean(yc * yc, axis=-1, keepdims=True)
    return yc * lax.rsqrt(var + LN_EPS) * g + b


def _shift_rows(cur, prev_tail, k):
    rolled = pltpu.roll(cur, k, axis=0)
    head_prev = pltpu.roll(prev_tail, k, axis=0)
    row = lax.broadcasted_iota(jnp.int32, prev_tail.shape, 0)
    head = jnp.where(row < k, head_prev, rolled[:SUBLANES])
    return jnp.concatenate([head, rolled[SUBLANES:]], axis=0)


def _causal_conv3(cur, prev_tail, w):
    return (w[0:1] * _shift_rows(cur, prev_tail, 2)
            + w[1:2] * _shift_rows(cur, prev_tail, 1)
            + w[2:3] * cur)


def _reset_at_sequence_start(ref):
    @pl.when(pl.program_id(1) == 0)
    def _():
        ref[...] = jnp.zeros_like(ref)


def _mixer_ab_kernel(x_ref, w_in_ref, ln_g_ref, ln_b_ref, w_s_ref, b_st_ref, pool_w_ref,
                     pool_scale_ref, w_out_ref, g_ref, b_ref, o_ref, pool_tail_ref, *, alpha):
    _reset_at_sequence_start(pool_tail_ref)
    x = x_ref[...]
    rows = x.shape[0]
    a_width = ln_g_ref.shape[1]
    groups, block, _ = w_s_ref.shape
    group_dim = a_width // groups
    n_blocks = rows // block

    proj = _dot(x.astype(_BF16), w_in_ref[...])

    uv = jax.nn.gelu(proj[:, :2 * a_width])
    u = uv[:, :a_width]
    v = _layer_norm(uv[:, a_width:], ln_g_ref[...], ln_b_ref[...]).astype(_BF16)
    chunk_i = lax.broadcasted_iota(jnp.int32, (block, block), 0) // CHUNK
    chunk_j = lax.broadcasted_iota(jnp.int32, (block, block), 1) // CHUNK
    mask = chunk_j <= chunk_i
    sp_groups = []
    for g in range(groups):
        w_m = jnp.where(mask, w_s_ref[g], 0.0).astype(_BF16)
        lanes = slice(g * group_dim, (g + 1) * group_dim)
        v_blocks = jnp.concatenate(
            [v[n * block:(n + 1) * block, lanes] for n in range(n_blocks)], axis=1)
        mixed = _dot(w_m, v_blocks) + b_st_ref[:, g:g + 1]
        sp_groups.append(jnp.concatenate(
            [mixed[:, n * group_dim:(n + 1) * group_dim] for n in range(n_blocks)], axis=0))
    y_a = u * jnp.concatenate(sp_groups, axis=1)

    pooled_in = proj[:, 2 * a_width:]
    tail_rows = pool_tail_ref.shape[0]
    hist = jnp.concatenate([pool_tail_ref[...], pooled_in], axis=0)
    pool_tail_ref[...] = pooled_in[rows - tail_rows:, :]
    pos = pl.program_id(1) * rows + lax.broadcasted_iota(jnp.int32, (rows, 1), 0)
    b_group_dim = pool_w_ref.shape[1]
    y_b_groups = []
    for g, window in enumerate(POOL_WINDOWS):
        lanes = slice(g * b_group_dim, (g + 1) * b_group_dim)
        s = hist[:, lanes]
        span = 1
        while span < window:
            s = s + pltpu.roll(s, span, axis=0)
            span *= 2
        inv_count = 1.0 / jnp.minimum(pos + 1, window).astype(_F32)
        pooled = s[tail_rows:, :] * inv_count - pooled_in[:, lanes]
        y_b_groups.append(_dot(pooled.astype(_BF16), pool_w_ref[g]))
    y_b = jnp.concatenate(y_b_groups, axis=1) * pool_scale_ref[...]

    mix = _dot(jnp.concatenate([y_a, y_b], axis=1).astype(_BF16), w_out_ref[...])
    o_ref[...] = _layer_norm(alpha * x + mix, g_ref[...], b_ref[...])


def _mixer_c_kernel(x_ref, w_in_ref, conv_w_ref, w_out_ref, g_ref, b_ref, o_ref, tail_ref, *, alpha):
    _reset_at_sequence_start(tail_ref)
    x = x_ref[...]
    rows = x.shape[0]
    width = w_out_ref.shape[0]
    proj = _dot(x.astype(_BF16), w_in_ref[...])
    gate_b = proj[:, :width]
    z = proj[:, width:2 * width] * proj[:, 2 * width:]
    prev_tail = tail_ref[...]
    tail_ref[...] = z[rows - SUBLANES:, :]
    y = gate_b * _causal_conv3(z, prev_tail, conv_w_ref[...])
    mix = _dot(y.astype(_BF16), w_out_ref[...])
    o_ref[...] = _layer_norm(alpha * x + mix, g_ref[...], b_ref[...])


def _kv_proj_kernel(mem_ref, w_kv_ref, kv_ref):
    kv_ref[0] = _dot(mem_ref[0].astype(_BF16), w_kv_ref[...]).astype(kv_ref.dtype)


def _cross_attend_kernel(x_ref, w_q_ref, w_o_ref, g_ref, b_ref, kv_ref, o_ref, *, alpha):
    x = x_ref[...]
    d = x.shape[1]
    head_dim = d // XATTN_HEADS
    q = _dot(x.astype(_BF16), w_q_ref[...])
    heads = []
    for h in range(XATTN_HEADS):
        lanes = slice(h * head_dim, (h + 1) * head_dim)
        k_h = kv_ref[0, :, h * head_dim:(h + 1) * head_dim]
        v_h = kv_ref[0, :, d + h * head_dim:d + (h + 1) * head_dim]
        scores = lax.dot_general(q[:, lanes].astype(_BF16), k_h, (((1,), (1,)), ((), ())),
                                 preferred_element_type=_F32) * (head_dim ** -0.5)
        e = jnp.exp(scores - jnp.max(scores, axis=-1, keepdims=True))
        probs = e * (1.0 / jnp.sum(e, axis=-1, keepdims=True))
        heads.append(_dot(probs.astype(_BF16), v_h))
    attn = _dot(jnp.concatenate(heads, axis=1).astype(_BF16), w_o_ref[...])
    o_ref[...] = _layer_norm(alpha * x + attn, g_ref[...], b_ref[...])


def _ffn_column_chunks(d_ff):
    widths = []
    left = d_ff
    while left > 0:
        w = min(left, 4 * MXU_COLUMNS)
        widths.append(w)
        left -= w
    starts = [sum(widths[:i]) for i in range(len(widths))]
    return tuple(zip(starts, widths))


def _conv_ffn_kernel(x_ref, w_up_ref, conv_w_ref, conv_b_ref, w_down_ref, g_ref, b_ref, o_ref,
                     tail_ref, *, alpha):
    _reset_at_sequence_start(tail_ref)
    x = x_ref[...]
    rows = x.shape[0]
    xb = x.astype(_BF16)
    d_ff = w_down_ref.shape[0]
    acc = None
    for start, width in _ffn_column_chunks(d_ff):
        cols = slice(start, start + width)
        gate_cols = slice(d_ff + start, d_ff + start + width)
        a = _dot(xb, w_up_ref[:, cols])
        gate = _dot(xb, w_up_ref[:, gate_cols])
        prev_tail = tail_ref[:, cols]
        tail_ref[:, cols] = gate[rows - SUBLANES:, :]
        gate = _causal_conv3(gate, prev_tail, conv_w_ref[:, cols]) + conv_b_ref[:, cols]
        hidden = (jax.nn.gelu(gate) * a).astype(_BF16)
        part = _dot(hidden, w_down_ref[cols, :])
        acc = part if acc is None else acc + part
    o_ref[...] = _layer_norm(alpha * x + acc, g_ref[...], b_ref[...])


def _resident(shape):
    zeros = (0,) * len(shape)
    return pl.BlockSpec(shape, lambda b, j: zeros, pipeline_mode=pl.Buffered(1))


def _sublayer_call(body, name, x2d, batch, params, scratch_shapes, per_batch=()):
    n_rows, d = x2d.shape
    seq = n_rows // batch
    rows = min(ROW_TILE, seq)
    assert seq % rows == 0 and rows % SGU_BLOCK == 0
    tiles = seq // rows
    row_spec = pl.BlockSpec((rows, d), lambda b, j: (b * tiles + j, 0))
    in_specs = [row_spec]
    operands = [x2d]
    for p in params:
        in_specs.append(_resident(p.shape))
        operands.append(p)
    for p in per_batch:
        block = (1,) + p.shape[1:]
        in_specs.append(pl.BlockSpec(block, lambda b, j, n=len(block): (b,) + (0,) * (n - 1)))
        operands.append(p)
    return pl.pallas_call(
        body,
        out_shape=jax.ShapeDtypeStruct(x2d.shape, x2d.dtype),
        grid=(batch, tiles),
        in_specs=in_specs,
        out_specs=row_spec,
        scratch_shapes=scratch_shapes,
        compiler_params=pltpu.CompilerParams(
            dimension_semantics=("arbitrary", "arbitrary"),
            vmem_limit_bytes=VMEM_LIMIT_BYTES),
        name=name,
    )(*operands)


def _row(v):
    return v.reshape(1, -1)


def _mixer_ab(x2d, batch, alpha, w_in, ln_g, ln_b, w_s, b_s, pool_w, pool_scale, w_out, g, b):
    assert all(w & (w - 1) == 0 for w in POOL_WINDOWS)
    b_width = pool_scale.shape[0]
    tail_rows = -(-max(POOL_WINDOWS) // SUBLANES) * SUBLANES
    params = [w_in.astype(_BF16), _row(ln_g), _row(ln_b), w_s, b_s.T, pool_w.astype(_BF16),
              _row(pool_scale), w_out.astype(_BF16), _row(g), _row(b)]
    return _sublayer_call(
        functools.partial(_mixer_ab_kernel, alpha=alpha), "mixer_ab", x2d, batch, params,
        [pltpu.VMEM((tail_rows, b_width), _F32)])


def _mixer_c(x2d, batch, alpha, w_in, conv_w, w_out, g, b):
    params = [w_in.astype(_BF16), conv_w, w_out.astype(_BF16), _row(g), _row(b)]
    return _sublayer_call(
        functools.partial(_mixer_c_kernel, alpha=alpha), "mixer_c", x2d, batch, params,
        [pltpu.VMEM((SUBLANES, w_out.shape[0]), _F32)])


def _kv_proj(mem, w_kv):
    batch, mem_len, d = mem.shape
    return pl.pallas_call(
        _kv_proj_kernel,
        out_shape=jax.ShapeDtypeStruct((batch, mem_len, w_kv.shape[1]), _BF16),
        grid=(batch,),
        in_specs=[pl.BlockSpec((1, mem_len, d), lambda b: (b, 0, 0)),
                  pl.BlockSpec(w_kv.shape, lambda b: (0, 0), pipeline_mode=pl.Buffered(1))],
        out_specs=pl.BlockSpec((1, mem_len, w_kv.shape[1]), lambda b: (b, 0, 0)),
        compiler_params=pltpu.CompilerParams(
            dimension_semantics=("arbitrary",), vmem_limit_bytes=VMEM_LIMIT_BYTES),
        name="kv_proj",
    )(mem, w_kv.astype(_BF16))


def _cross_attend(x2d, batch, alpha, mem, w_q, w_kv, w_o, g, b):
    params = [w_q.astype(_BF16), w_o.astype(_BF16), _row(g), _row(b)]
    return _sublayer_call(
        functools.partial(_cross_attend_kernel, alpha=alpha), "cross_attend", x2d, batch, params,
        [], per_batch=[_kv_proj(mem, w_kv)])


def _conv_ffn(x2d, batch, alpha, w_up, conv_w, conv_b, w_down, g, b):
    params = [w_up.astype(_BF16), conv_w, _row(conv_b), w_down.astype(_BF16), _row(g), _row(b)]
    return _sublayer_call(
        functools.partial(_conv_ffn_kernel, alpha=alpha), "conv_ffn", x2d, batch, params,
        [pltpu.VMEM((SUBLANES, w_down.shape[0]), _F32)])


def kernel(x, mem, ab_w_in, sgu_ln_g, sgu_ln_b, sgu_w, sgu_b, pool_w, pool_scale, ab_w_out, c_w_in, c_conv_w, c_w_out, ln_mix_g, ln_mix_b, xa_wq, xa_wkv, xa_wo, ln_xa_g, ln_xa_b, ffn_w_up, ffn_conv_w, ffn_conv_b, ffn_w_down, ln_ffn_g, ln_ffn_b):
    batch, seq, d = x.shape
    depth = ln_mix_g.shape[0]
    alpha = (2 * depth) ** 0.25
    h = x.reshape(batch * seq, d)
    for layer in range(depth):
        j = layer // 2
        if layer % 2 == 0:
            h = _mixer_ab(h, batch, alpha, ab_w_in[j], sgu_ln_g[j], sgu_ln_b[j], sgu_w[j], sgu_b[j],
                          pool_w[j], pool_scale[j], ab_w_out[j], ln_mix_g[layer], ln_mix_b[layer])
        else:
            h = _mixer_c(h, batch, alpha, c_w_in[j], c_conv_w[j], c_w_out[j],
                         ln_mix_g[layer], ln_mix_b[layer])
        h = _cross_attend(h, batch, alpha, mem, xa_wq[layer], xa_wkv[layer], xa_wo[layer],
                          ln_xa_g[layer], ln_xa_b[layer])
        h = _conv_ffn(h, batch, alpha, ffn_w_up[layer], ffn_conv_w[layer], ffn_conv_b[layer],
                      ffn_w_down[layer], ln_ffn_g[layer], ln_ffn_b[layer])
    return h.reshape(batch, seq, d)
```

```python
import functools

import jax
import jax.numpy as jnp
from jax import lax
from jax.experimental import pallas as pl
from jax.experimental.pallas import tpu as pltpu

LN_EPS = 1e-5
CHUNK = 64
SGU_BLOCK = 128
POOL_WINDOWS = (2, 4, 8, 16)
XATTN_HEADS = 4

SUBLANES = 8
BF16_SUBLANES = 16
LANES = 128
MXU_COLUMNS = 256
ROW_TILE = 512
FFN_CHUNK_COLUMNS = 4 * MXU_COLUMNS
VMEM_LIMIT_BYTES = 56 * 1024 * 1024

_F32 = jnp.float32
_BF16 = jnp.bfloat16


def _dot(a, b):
    return jnp.dot(a, b, preferred_element_type=_F32)


def _layer_norm(y, g, b):
    mu = jnp.mean(y, axis=-1, keepdims=True)
    yc = y - mu
    var = jnp.mean(yc * yc, axis=-1, keepdims=True)
    return yc * lax.rsqrt(var + LN_EPS) * g + b


def _shift_rows(cur, prev_tail, k):
    rolled = pltpu.roll(cur, k, axis=0)
    head_prev = pltpu.roll(prev_tail, k, axis=0)
    row = lax.broadcasted_iota(jnp.int32, prev_tail.shape, 0)
    head = jnp.where(row < k, head_prev, rolled[:SUBLANES])
    return jnp.concatenate([head, rolled[SUBLANES:]], axis=0)


def _causal_conv3(cur, prev_tail, w):
    return (w[0:1] * _shift_rows(cur, prev_tail, 2)
            + w[1:2] * _shift_rows(cur, prev_tail, 1)
            + w[2:3] * cur)


def _ready_only_after(value, producer):
    bits = lax.bitcast_convert_type(producer, jnp.uint32)
    rows, cols = bits.shape
    folded = bits[:, :LANES]
    for c in range(1, cols // LANES):
        folded = folded | bits[:, c * LANES:(c + 1) * LANES]
    while rows > SUBLANES:
        rows //= 2
        folded = folded[:rows] | folded[rows:]
    half_word = jnp.uint32(16)
    zero = lax.shift_right_logical(lax.shift_right_logical(folded, half_word), half_word)
    first = lax.bitcast_convert_type(value[:SUBLANES, :LANES], jnp.uint32) + zero
    first = lax.bitcast_convert_type(first, value.dtype)
    top = jnp.concatenate([first, value[:SUBLANES, LANES:]], axis=1)
    return jnp.concatenate([top, value[SUBLANES:]], axis=0)


def _late_lhs(x, xb, producer):
    head = _ready_only_after(x[:BF16_SUBLANES], producer).astype(_BF16)
    return jnp.concatenate([head, xb[BF16_SUBLANES:]], axis=0)


def _mixer_ab_mix(x, prev_out, tile_in_seq, w_in_ref, ln_g_ref, ln_b_ref, w_s_ref, b_st_ref,
                  pool_w_ref, pool_scale_ref, w_out_ref, pool_tail_ref):
    rows = x.shape[0]
    a_width = ln_g_ref.shape[1]
    groups, block, _ = w_s_ref.shape
    group_dim = a_width // groups
    n_blocks = rows // block

    xb = x.astype(_BF16)
    proj_a = _dot(xb, w_in_ref[:, :2 * a_width])
    pooled_in = _dot(_late_lhs(x, xb, prev_out), w_in_ref[:, 2 * a_width:])

    uv = jax.nn.gelu(proj_a)
    u = uv[:, :a_width]
    v = _layer_norm(uv[:, a_width:], ln_g_ref[...], ln_b_ref[...]).astype(_BF16)
    chunk_i = lax.broadcasted_iota(jnp.int32, (block, block), 0) // CHUNK
    chunk_j = lax.broadcasted_iota(jnp.int32, (block, block), 1) // CHUNK
    mask = chunk_j <= chunk_i
    sp_groups = []
    for g in range(groups):
        w_m = jnp.where(mask, w_s_ref[g], 0.0).astype(_BF16)
        lanes = slice(g * group_dim, (g + 1) * group_dim)
        v_blocks = jnp.concatenate(
            [v[n * block:(n + 1) * block, lanes] for n in range(n_blocks)], axis=1)
        mixed = _dot(w_m, v_blocks) + b_st_ref[:, g:g + 1]
        sp_groups.append(jnp.concatenate(
            [mixed[:, n * group_dim:(n + 1) * group_dim] for n in range(n_blocks)], axis=0))
    y_a = u * jnp.concatenate(sp_groups, axis=1)

    tail_rows = pool_tail_ref.shape[0]
    hist = jnp.concatenate([pool_tail_ref[...], pooled_in], axis=0)
    pool_tail_ref[...] = pooled_in[rows - tail_rows:, :]
    pos = tile_in_seq * rows + lax.broadcasted_iota(jnp.int32, (rows, 1), 0)
    b_group_dim = pool_w_ref.shape[1]
    y_b_groups = []
    for g, window in enumerate(POOL_WINDOWS):
        lanes = slice(g * b_group_dim, (g + 1) * b_group_dim)
        s = hist[:, lanes]
        span = 1
        while span < window:
            s = s + pltpu.roll(s, span, axis=0)
            span *= 2
        inv_count = 1.0 / jnp.minimum(pos + 1, window).astype(_F32)
        pooled = s[tail_rows:, :] * inv_count - pooled_in[:, lanes]
        y_b_groups.append(_dot(pooled.astype(_BF16), pool_w_ref[g]))
    y_b = jnp.concatenate(y_b_groups, axis=1) * pool_scale_ref[...]

    return _dot(jnp.concatenate([y_a, y_b], axis=1).astype(_BF16), w_out_ref[...])


def _mixer_c_mix(x, prev_out, tile_in_seq, w_in_ref, conv_w_ref, w_out_ref, tail_ref):
    rows = x.shape[0]
    width = w_out_ref.shape[0]
    xb = x.astype(_BF16)
    gate_b = _dot(xb, w_in_ref[:, :width])
    proj = _dot(_late_lhs(x, xb, prev_out), w_in_ref[:, width:])
    z = proj[:, :width] * proj[:, width:]
    prev_tail = tail_ref[...]
    tail_ref[...] = z[rows - SUBLANES:, :]
    y = gate_b * _causal_conv3(z, prev_tail, conv_w_ref[...])
    return _dot(y.astype(_BF16), w_out_ref[...])


def _kv_proj_kernel(mem_ref, w_kv_ref, kv_ref):
    kv_ref[0] = _dot(mem_ref[0].astype(_BF16), w_kv_ref[...]).astype(kv_ref.dtype)


def _cross_attend_mix(x, prev_out, tile_in_seq, w_q_ref, w_o_ref, kv_ref):
    d = x.shape[1]
    head_dim = d // XATTN_HEADS
    q = _dot(x.astype(_BF16), w_q_ref[...])
    heads = []
    for h in range(XATTN_HEADS):
        lanes = slice(h * head_dim, (h + 1) * head_dim)
        k_h = kv_ref[0, :, h * head_dim:(h + 1) * head_dim]
        v_h = kv_ref[0, :, d + h * head_dim:d + (h + 1) * head_dim]
        scores = lax.dot_general(q[:, lanes].astype(_BF16), k_h, (((1,), (1,)), ((), ())),
                                 preferred_element_type=_F32) * (head_dim ** -0.5)
        e = jnp.exp(scores - jnp.max(scores, axis=-1, keepdims=True))
        probs = e * (1.0 / jnp.sum(e, axis=-1, keepdims=True))
        heads.append(_dot(probs.astype(_BF16), v_h))
    return _dot(jnp.concatenate(heads, axis=1).astype(_BF16), w_o_ref[...])


def _ffn_column_chunks(d_ff):
    widths = []
    left = d_ff
    while left > 0:
        w = min(left, FFN_CHUNK_COLUMNS)
        widths.append(w)
        left -= w
    starts = [sum(widths[:i]) for i in range(len(widths))]
    return tuple(zip(starts, widths))


def _conv_ffn_mix(x, prev_out, tile_in_seq, w_up_ref, conv_w_ref, conv_b_ref, w_down_ref, tail_ref):
    rows = x.shape[0]
    xb = x.astype(_BF16)
    d_ff = w_down_ref.shape[1]
    acc = None
    for start, width in _ffn_column_chunks(d_ff):
        cols = slice(start, start + width)
        gate_cols = slice(d_ff + start, d_ff + start + width)
        a = _dot(xb, w_up_ref[:, cols])
        gate = _dot(xb, w_up_ref[:, gate_cols])
        prev_tail = tail_ref[:, cols]
        tail_ref[:, cols] = gate[rows - SUBLANES:, :]
        gate = _causal_conv3(gate, prev_tail, conv_w_ref[:, cols]) + conv_b_ref[:, cols]
        hidden = (jax.nn.gelu(gate) * a).astype(_BF16)
        part = lax.dot_general(w_down_ref[:, cols], hidden, (((1,), (1,)), ((), ())),
                               preferred_element_type=_F32)
        acc = part if acc is None else acc + part
    return acc.T


def _sublayer_kernel(x_ref, g_ref, b_ref, *refs, mix, n_inputs, n_carried, alpha, n_tiles,
                     tiles_per_seq):
    in_refs = refs[:n_inputs]
    o_ref, pre_norm_ref = refs[n_inputs], refs[n_inputs + 1]
    scratch_refs = refs[n_inputs + 2:]
    s = pl.program_id(0)
    tile_in_seq = s % tiles_per_seq

    @pl.when(s == 0)
    def _():
        pre_norm_ref[...] = jnp.zeros_like(pre_norm_ref)

    @pl.when(tile_in_seq == 0)
    def _():
        for ref in scratch_refs[:n_carried]:
            ref[...] = jnp.zeros_like(ref)

    def finish_previous_tile():
        prev_out = _layer_norm(pre_norm_ref[...], g_ref[...], b_ref[...])
        o_ref[...] = prev_out
        return prev_out

    @pl.when(s < n_tiles)
    def _():
        prev_out = finish_previous_tile()
        x = x_ref[...]
        pre_norm_ref[...] = alpha * x + mix(x, prev_out, tile_in_seq, *in_refs, *scratch_refs)

    @pl.when(s == n_tiles)
    def _():
        finish_previous_tile()


def _resident(shape):
    zeros = (0,) * len(shape)
    return pl.BlockSpec(shape, lambda s: zeros, pipeline_mode=pl.Buffered(1))


def _sublayer_call(mix, name, x2d, batch, alpha, g, b, params, carried_shapes, per_batch=()):
    n_rows, d = x2d.shape
    seq = n_rows // batch
    rows = min(ROW_TILE, seq)
    assert seq % rows == 0 and rows % SGU_BLOCK == 0
    tiles_per_seq = seq // rows
    n_tiles = n_rows // rows
    last = n_tiles - 1
    operands = [x2d, g.reshape(1, d), b.reshape(1, d)]
    in_specs = [pl.BlockSpec((rows, d), lambda s: (jnp.minimum(s, last), 0)),
                _resident((1, d)), _resident((1, d))]
    for p in params:
        in_specs.append(_resident(p.shape))
        operands.append(p)
    for p in per_batch:
        block = (1,) + p.shape[1:]
        in_specs.append(pl.BlockSpec(
            block, lambda s, n=len(block): (jnp.minimum(s, last) // tiles_per_seq,) + (0,) * (n - 1)))
        operands.append(p)
    body = functools.partial(
        _sublayer_kernel, mix=mix, n_inputs=len(params) + len(per_batch),
        n_carried=len(carried_shapes), alpha=alpha, n_tiles=n_tiles, tiles_per_seq=tiles_per_seq)
    return pl.pallas_call(
        body,
        out_shape=jax.ShapeDtypeStruct(x2d.shape, x2d.dtype),
        grid=(n_tiles + 1,),
        in_specs=in_specs,
        out_specs=pl.BlockSpec((rows, d), lambda s: (jnp.maximum(s - 1, 0), 0)),
        scratch_shapes=[pltpu.VMEM((rows, d), _F32)] + [pltpu.VMEM(shape, _F32) for shape in carried_shapes],
        compiler_params=pltpu.CompilerParams(
            dimension_semantics=("arbitrary",), vmem_limit_bytes=VMEM_LIMIT_BYTES),
        name=name,
    )(*operands)


def _row(v):
    return v.reshape(1, -1)


def _mixer_ab(x2d, batch, alpha, w_in, ln_g, ln_b, w_s, b_s, pool_w, pool_scale, w_out, g, b):
    assert all(w & (w - 1) == 0 for w in POOL_WINDOWS)
    tail_rows = -(-max(POOL_WINDOWS) // SUBLANES) * SUBLANES
    params = [w_in.astype(_BF16), _row(ln_g), _row(ln_b), w_s, b_s.T, pool_w.astype(_BF16),
              _row(pool_scale), w_out.astype(_BF16)]
    return _sublayer_call(_mixer_ab_mix, "mixer_ab", x2d, batch, alpha, g, b, params,
                          [(tail_rows, pool_scale.shape[0])])


def _mixer_c(x2d, batch, alpha, w_in, conv_w, w_out, g, b):
    params = [w_in.astype(_BF16), conv_w, w_out.astype(_BF16)]
    return _sublayer_call(_mixer_c_mix, "mixer_c", x2d, batch, alpha, g, b, params,
                          [(SUBLANES, w_out.shape[0])])


def _kv_proj(mem, w_kv):
    batch, mem_len, d = mem.shape
    return pl.pallas_call(
        _kv_proj_kernel,
        out_shape=jax.ShapeDtypeStruct((batch, mem_len, w_kv.shape[1]), _BF16),
        grid=(batch,),
        in_specs=[pl.BlockSpec((1, mem_len, d), lambda b: (b, 0, 0)),
                  pl.BlockSpec(w_kv.shape, lambda b: (0, 0), pipeline_mode=pl.Buffered(1))],
        out_specs=pl.BlockSpec((1, mem_len, w_kv.shape[1]), lambda b: (b, 0, 0)),
        compiler_params=pltpu.CompilerParams(
            dimension_semantics=("arbitrary",), vmem_limit_bytes=VMEM_LIMIT_BYTES),
        name="kv_proj",
    )(mem, w_kv.astype(_BF16))


def _cross_attend(x2d, batch, alpha, mem, w_q, w_kv, w_o, g, b):
    params = [w_q.astype(_BF16), w_o.astype(_BF16)]
    return _sublayer_call(_cross_attend_mix, "cross_attend", x2d, batch, alpha, g, b, params,
                          [], per_batch=[_kv_proj(mem, w_kv)])


def _conv_ffn(x2d, batch, alpha, w_up, conv_w, conv_b, w_down, g, b):
    params = [w_up.astype(_BF16), conv_w, _row(conv_b), w_down.T.astype(_BF16)]
    return _sublayer_call(_conv_ffn_mix, "conv_ffn", x2d, batch, alpha, g, b, params,
                          [(SUBLANES, w_down.shape[0])])


def kernel(x, mem, ab_w_in, sgu_ln_g, sgu_ln_b, sgu_w, sgu_b, pool_w, pool_scale, ab_w_out, c_w_in, c_conv_w, c_w_out, ln_mix_g, ln_mix_b, xa_wq, xa_wkv, xa_wo, ln_xa_g, ln_xa_b, ffn_w_up, ffn_conv_w, ffn_conv_b, ffn_w_down, ln_ffn_g, ln_ffn_b):
    batch, seq, d = x.shape
    depth = ln_mix_g.shape[0]
    alpha = (2 * depth) ** 0.25
    h = x.reshape(batch * seq, d)
    for layer in range(depth):
        j = layer // 2
        if layer % 2 == 0:
            h = _mixer_ab(h, batch, alpha, ab_w_in[j], sgu_ln_g[j], sgu_ln_b[j], sgu_w[j], sgu_b[j],
                          pool_w[j], pool_scale[j], ab_w_out[j], ln_mix_g[layer], ln_mix_b[layer])
        else:
            h = _mixer_c(h, batch, alpha, c_w_in[j], c_conv_w[j], c_w_out[j],
                         ln_mix_g[layer], ln_mix_b[layer])
        h = _cross_attend(h, batch, alpha, mem, xa_wq[layer], xa_wkv[layer], xa_wo[layer],
                          ln_xa_g[layer], ln_xa_b[layer])
        h = _conv_ffn(h, batch, alpha, ffn_w_up[layer], ffn_conv_w[layer], ffn_conv_b[layer],
                      ffn_w_down[layer], ln_ffn_g[layer], ln_ffn_b[layer])
    return h.reshape(batch, seq, d)
```

```python
import functools

import jax
import jax.numpy as jnp
from jax import lax
from jax.experimental import pallas as pl
from jax.experimental.pallas import tpu as pltpu

LN_EPS = 1e-5
CHUNK = 64
SGU_BLOCK = 128
POOL_WINDOWS = (2, 4, 8, 16)
XATTN_HEADS = 4

SUBLANES = 8
BF16_SUBLANES = 16
LANES = 128
MXU_COLUMNS = 256
ROW_TILE = 512
FFN_CHUNK_COLUMNS = 4 * MXU_COLUMNS
VMEM_LIMIT_BYTES = 56 * 1024 * 1024

_F32 = jnp.float32
_BF16 = jnp.bfloat16


def _dot(a, b):
    return jnp.dot(a, b, preferred_element_type=_F32)


def _layer_norm(y, g, b):
    mu = jnp.mean(y, axis=-1, keepdims=True)
    yc = y - mu
    var = jnp.mean(yc * yc, axis=-1, keepdims=True)
    return yc * lax.rsqrt(var + LN_EPS) * g + b


def _shift_rows(cur, prev_tail, k):
    rolled = pltpu.roll(cur, k, axis=0)
    head_prev = pltpu.roll(prev_tail, k, axis=0)
    row = lax.broadcasted_iota(jnp.int32, prev_tail.shape, 0)
    head = jnp.where(row < k, head_prev, rolled[:SUBLANES])
    return jnp.concatenate([head, rolled[SUBLANES:]], axis=0)


def _causal_conv3(cur, prev_tail, w):
    return (w[0:1] * _shift_rows(cur, prev_tail, 2)
            + w[1:2] * _shift_rows(cur, prev_tail, 1)
            + w[2:3] * cur)


def _ready_only_after(value, producer):
    bits = lax.bitcast_convert_type(producer, jnp.uint32)
    rows, cols = bits.shape
    folded = bits[:, :LANES]
    for c in range(1, cols // LANES):
        folded = folded | bits[:, c * LANES:(c + 1) * LANES]
    while rows > SUBLANES:
        rows //= 2
        folded = folded[:rows] | folded[rows:]
    half_word = jnp.uint32(16)
    zero = lax.shift_right_logical(lax.shift_right_logical(folded, half_word), half_word)
    first = lax.bitcast_convert_type(value[:SUBLANES, :LANES], jnp.uint32) + zero
    first = lax.bitcast_convert_type(first, value.dtype)
    top = jnp.concatenate([first, value[:SUBLANES, LANES:]], axis=1)
    return jnp.concatenate([top, value[SUBLANES:]], axis=0)


def _late_lhs(x, xb, producer):
    head = _ready_only_after(x[:BF16_SUBLANES], producer).astype(_BF16)
    return jnp.concatenate([head, xb[BF16_SUBLANES:]], axis=0)


def _mixer_ab_mix(x, prev_out, tile_in_seq, w_in_ref, ln_g_ref, ln_b_ref, w_s_ref, b_st_ref,
                  pool_w_ref, pool_scale_ref, w_out_ref, pool_tail_ref):
    rows = x.shape[0]
    a_width = ln_g_ref.shape[1]
    groups, block, _ = w_s_ref.shape
    group_dim = a_width // groups
    n_blocks = rows // block

    xb = x.astype(_BF16)
    proj_a = _dot(xb, w_in_ref[:, :2 * a_width])
    pooled_in = _dot(_late_lhs(x, xb, prev_out), w_in_ref[:, 2 * a_width:])

    uv = jax.nn.gelu(proj_a)
    u = uv[:, :a_width]
    v = _layer_norm(uv[:, a_width:], ln_g_ref[...], ln_b_ref[...]).astype(_BF16)
    chunk_i = lax.broadcasted_iota(jnp.int32, (block, block), 0) // CHUNK
    chunk_j = lax.broadcasted_iota(jnp.int32, (block, block), 1) // CHUNK
    mask = chunk_j <= chunk_i
    sp_groups = []
    for g in range(groups):
        w_m = jnp.where(mask, w_s_ref[g], 0.0).astype(_BF16)
        lanes = slice(g * group_dim, (g + 1) * group_dim)
        v_blocks = jnp.concatenate(
            [v[n * block:(n + 1) * block, lanes] for n in range(n_blocks)], axis=1)
        mixed = _dot(w_m, v_blocks) + b_st_ref[:, g:g + 1]
        sp_groups.append(jnp.concatenate(
            [mixed[:, n * group_dim:(n + 1) * group_dim] for n in range(n_blocks)], axis=0))
    y_a = u * jnp.concatenate(sp_groups, axis=1)

    tail_rows = pool_tail_ref.shape[0]
    hist = jnp.concatenate([pool_tail_ref[...], pooled_in], axis=0)
    pool_tail_ref[...] = pooled_in[rows - tail_rows:, :]
    pos = tile_in_seq * rows + lax.broadcasted_iota(jnp.int32, (rows, 1), 0)
    b_group_dim = pool_w_ref.shape[1]
    y_b_groups = []
    for g, window in enumerate(POOL_WINDOWS):
        lanes = slice(g * b_group_dim, (g + 1) * b_group_dim)
        s = hist[:, lanes]
        span = 1
        while span < window:
            s = s + pltpu.roll(s, span, axis=0)
            span *= 2
        inv_count = 1.0 / jnp.minimum(pos + 1, window).astype(_F32)
        pooled = s[tail_rows:, :] * inv_count - pooled_in[:, lanes]
        y_b_groups.append(_dot(pooled.astype(_BF16), pool_w_ref[g]))
    y_b = jnp.concatenate(y_b_groups, axis=1) * pool_scale_ref[...]

    return _dot(jnp.concatenate([y_a, y_b], axis=1).astype(_BF16), w_out_ref[...])


def _mixer_c_mix(x, prev_out, tile_in_seq, w_in_ref, conv_w_ref, w_out_ref, tail_ref):
    rows = x.shape[0]
    width = w_out_ref.shape[0]
    xb = x.astype(_BF16)
    gate_b = _dot(xb, w_in_ref[:, :width])
    proj = _dot(_late_lhs(x, xb, prev_out), w_in_ref[:, width:])
    z = proj[:, :width] * proj[:, width:]
    prev_tail = tail_ref[...]
    tail_ref[...] = z[rows - SUBLANES:, :]
    y = gate_b * _causal_conv3(z, prev_tail, conv_w_ref[...])
    return _dot(y.astype(_BF16), w_out_ref[...])


def _kv_proj_kernel(mem_ref, w_kv_ref, kv_ref):
    kv_ref[0] = _dot(mem_ref[0].astype(_BF16), w_kv_ref[...]).astype(kv_ref.dtype)


def _cross_attend_mix(x, prev_out, tile_in_seq, w_q_ref, w_o_ref, kv_ref):
    d = x.shape[1]
    head_dim = d // XATTN_HEADS
    xb = x.astype(_BF16)

    def head_lanes(h):
        return slice(h * head_dim, (h + 1) * head_dim)

    def project_q(h):
        return _dot(xb, w_q_ref[:, head_lanes(h)]).astype(_BF16)

    def softmax_scores(h, q_h):
        k_h = kv_ref[0, :, head_lanes(h)]
        scores = lax.dot_general(q_h, k_h, (((1,), (1,)), ((), ())),
                                 preferred_element_type=_F32) * (head_dim ** -0.5)
        e = jnp.exp(scores - jnp.max(scores, axis=-1, keepdims=True))
        return (e * (1.0 / jnp.sum(e, axis=-1, keepdims=True))).astype(_BF16)

    def attend(h, probs):
        v_h = kv_ref[0, :, d + h * head_dim:d + (h + 1) * head_dim]
        return _dot(probs, v_h).astype(_BF16)

    def project_out(h, o_h):
        return _dot(o_h, w_o_ref[head_lanes(h), :])

    stages = (project_q, softmax_scores, attend, project_out)
    in_flight = {}
    out = None
    for i in range(XATTN_HEADS + len(stages) - 1):
        for depth, stage in enumerate(stages):
            h = i - depth
            if 0 <= h < XATTN_HEADS:
                in_flight[h] = stage(h) if depth == 0 else stage(h, in_flight[h])
                if depth == len(stages) - 1:
                    part = in_flight.pop(h)
                    out = part if out is None else out + part
    return out


def _ffn_column_chunks(d_ff):
    widths = []
    left = d_ff
    while left > 0:
        w = min(left, FFN_CHUNK_COLUMNS) if left > FFN_CHUNK_COLUMNS else max(MXU_COLUMNS, left // 2 // MXU_COLUMNS * MXU_COLUMNS)
        w = min(w, left)
        widths.append(w)
        left -= w
    starts = [sum(widths[:i]) for i in range(len(widths))]
    return tuple(zip(starts, widths))


def _conv_ffn_mix(x, prev_out, tile_in_seq, w_up_ref, conv_w_ref, conv_b_ref, w_down_ref, tail_ref):
    rows = x.shape[0]
    xb = x.astype(_BF16)
    d_ff = w_down_ref.shape[1]

    def up_project(start, width):
        a = _dot(xb, w_up_ref[:, start:start + width])
        gate = _dot(xb, w_up_ref[:, d_ff + start:d_ff + start + width])
        return a, gate

    def activate(start, width, a, gate):
        cols = slice(start, start + width)
        prev_tail = tail_ref[:, cols]
        tail_ref[:, cols] = gate[rows - SUBLANES:, :]
        gate = _causal_conv3(gate, prev_tail, conv_w_ref[:, cols]) + conv_b_ref[:, cols]
        return (jax.nn.gelu(gate) * a).astype(_BF16)

    def down_project(start, width, hidden):
        return lax.dot_general(w_down_ref[:, start:start + width], hidden, (((1,), (1,)), ((), ())),
                               preferred_element_type=_F32)

    acc = None
    pending = None
    for start, width in _ffn_column_chunks(d_ff):
        a, gate = up_project(start, width)
        if pending is not None:
            part = down_project(*pending)
            acc = part if acc is None else acc + part
        pending = (start, width, activate(start, width, a, gate))
    part = down_project(*pending)
    acc = part if acc is None else acc + part
    return acc.T


def _sublayer_kernel(x_ref, g_ref, b_ref, *refs, mix, n_inputs, n_carried, alpha, n_tiles,
                     tiles_per_seq):
    in_refs = refs[:n_inputs]
    o_ref, pre_norm_ref = refs[n_inputs], refs[n_inputs + 1]
    scratch_refs = refs[n_inputs + 2:]
    s = pl.program_id(0)
    tile_in_seq = s % tiles_per_seq

    @pl.when(s == 0)
    def _():
        pre_norm_ref[...] = jnp.zeros_like(pre_norm_ref)

    @pl.when(tile_in_seq == 0)
    def _():
        for ref in scratch_refs[:n_carried]:
            ref[...] = jnp.zeros_like(ref)

    def finish_previous_tile():
        prev_out = _layer_norm(pre_norm_ref[...], g_ref[...], b_ref[...])
        o_ref[...] = prev_out
        return prev_out

    @pl.when(s < n_tiles)
    def _():
        prev_out = finish_previous_tile()
        x = x_ref[...]
        pre_norm_ref[...] = alpha * x + mix(x, prev_out, tile_in_seq, *in_refs, *scratch_refs)

    @pl.when(s == n_tiles)
    def _():
        finish_previous_tile()


def _resident(param):
    if isinstance(param, tuple):
        stack, layer = param
        index = (layer,) + (0,) * (stack.ndim - 1)
        return stack, pl.BlockSpec((None,) + stack.shape[1:], lambda s: index,
                                   pipeline_mode=pl.Buffered(1))
    zeros = (0,) * param.ndim
    return param, pl.BlockSpec(param.shape, lambda s: zeros, pipeline_mode=pl.Buffered(1))


def _sublayer_call(mix, name, x2d, batch, alpha, g, b, params, carried_shapes, per_batch=()):
    n_rows, d = x2d.shape
    seq = n_rows // batch
    rows = min(ROW_TILE, seq)
    assert seq % rows == 0 and rows % SGU_BLOCK == 0
    tiles_per_seq = seq // rows
    n_tiles = n_rows // rows
    last = n_tiles - 1
    operands = [x2d]
    in_specs = [pl.BlockSpec((rows, d), lambda s: (jnp.minimum(s, last), 0))]
    for p in [g.reshape(1, d), b.reshape(1, d)] + list(params):
        operand, spec = _resident(p)
        operands.append(operand)
        in_specs.append(spec)
    for p in per_batch:
        block = (1,) + p.shape[1:]
        in_specs.append(pl.BlockSpec(
            block, lambda s, n=len(block): (jnp.minimum(s, last) // tiles_per_seq,) + (0,) * (n - 1)))
        operands.append(p)
    body = functools.partial(
        _sublayer_kernel, mix=mix, n_inputs=len(params) + len(per_batch),
        n_carried=len(carried_shapes), alpha=alpha, n_tiles=n_tiles, tiles_per_seq=tiles_per_seq)
    return pl.pallas_call(
        body,
        out_shape=jax.ShapeDtypeStruct(x2d.shape, x2d.dtype),
        grid=(n_tiles + 1,),
        in_specs=in_specs,
        out_specs=pl.BlockSpec((rows, d), lambda s: (jnp.maximum(s - 1, 0), 0)),
        scratch_shapes=[pltpu.VMEM((rows, d), _F32)] + [pltpu.VMEM(shape, _F32) for shape in carried_shapes],
        compiler_params=pltpu.CompilerParams(
            dimension_semantics=("arbitrary",), vmem_limit_bytes=VMEM_LIMIT_BYTES),
        name=name,
    )(*operands)


def _row(v):
    return v.reshape(1, -1)


def _mixer_ab(x2d, batch, alpha, w_in, ln_g, ln_b, w_s, b_s, pool_w, pool_scale, w_out, g, b):
    assert all(w & (w - 1) == 0 for w in POOL_WINDOWS)
    tail_rows = -(-max(POOL_WINDOWS) // SUBLANES) * SUBLANES
    params = [w_in, _row(ln_g), _row(ln_b), w_s, b_s.T, pool_w, _row(pool_scale), w_out]
    return _sublayer_call(_mixer_ab_mix, "mixer_ab", x2d, batch, alpha, g, b, params,
                          [(tail_rows, pool_scale.shape[0])])


def _mixer_c(x2d, batch, alpha, w_in, conv_w, w_out, g, b):
    return _sublayer_call(_mixer_c_mix, "mixer_c", x2d, batch, alpha, g, b, [w_in, conv_w, w_out],
                          [(SUBLANES, conv_w.shape[1])])


def _kv_proj(mem, w_kv):
    batch, mem_len, d = mem.shape
    w_kv, w_kv_spec = _resident(w_kv)
    kv_width = w_kv.shape[-1]
    return pl.pallas_call(
        _kv_proj_kernel,
        out_shape=jax.ShapeDtypeStruct((batch, mem_len, kv_width), _BF16),
        grid=(batch,),
        in_specs=[pl.BlockSpec((1, mem_len, d), lambda b: (b, 0, 0)), w_kv_spec],
        out_specs=pl.BlockSpec((1, mem_len, kv_width), lambda b: (b, 0, 0)),
        compiler_params=pltpu.CompilerParams(
            dimension_semantics=("arbitrary",), vmem_limit_bytes=VMEM_LIMIT_BYTES),
        name="kv_proj",
    )(mem, w_kv)


def _cross_attend(x2d, batch, alpha, mem, w_q, w_kv, w_o, g, b):
    return _sublayer_call(_cross_attend_mix, "cross_attend", x2d, batch, alpha, g, b, [w_q, w_o],
                          [], per_batch=[_kv_proj(mem, w_kv)])


def _conv_ffn(x2d, batch, alpha, w_up, conv_w, conv_b, w_down_t, g, b):
    params = [w_up, conv_w, _row(conv_b), w_down_t]
    return _sublayer_call(_conv_ffn_mix, "conv_ffn", x2d, batch, alpha, g, b, params,
                          [(SUBLANES, conv_w.shape[1])])


def kernel(x, mem, ab_w_in, sgu_ln_g, sgu_ln_b, sgu_w, sgu_b, pool_w, pool_scale, ab_w_out, c_w_in, c_conv_w, c_w_out, ln_mix_g, ln_mix_b, xa_wq, xa_wkv, xa_wo, ln_xa_g, ln_xa_b, ffn_w_up, ffn_conv_w, ffn_conv_b, ffn_w_down, ln_ffn_g, ln_ffn_b):
    batch, seq, d = x.shape
    depth = ln_mix_g.shape[0]
    alpha = (2 * depth) ** 0.25
    ab_w_in, pool_w, ab_w_out, c_w_in, c_w_out, xa_wq, xa_wkv, xa_wo, ffn_w_up = (
        w.astype(_BF16) for w in (ab_w_in, pool_w, ab_w_out, c_w_in, c_w_out, xa_wq, xa_wkv, xa_wo,
                                  ffn_w_up))
    ffn_w_down_t = jnp.swapaxes(ffn_w_down, 1, 2).astype(_BF16)
    h = x.reshape(batch * seq, d)
    for layer in range(depth):
        j = layer // 2
        if layer % 2 == 0:
            h = _mixer_ab(h, batch, alpha, (ab_w_in, j), sgu_ln_g[j], sgu_ln_b[j], sgu_w[j], sgu_b[j],
                          (pool_w, j), pool_scale[j], (ab_w_out, j), ln_mix_g[layer], ln_mix_b[layer])
        else:
            h = _mixer_c(h, batch, alpha, (c_w_in, j), c_conv_w[j], (c_w_out, j),
                         ln_mix_g[layer], ln_mix_b[layer])
        h = _cross_attend(h, batch, alpha, mem, (xa_wq, layer), (xa_wkv, layer), (xa_wo, layer),
                          ln_xa_g[layer], ln_xa_b[layer])
        h = _conv_ffn(h, batch, alpha, (ffn_w_up, layer), ffn_conv_w[layer], ffn_conv_b[layer],
                      (ffn_w_down_t, layer), ln_ffn_g[layer], ln_ffn_b[layer])
    return h.reshape(batch, seq, d)
```
